```python
import math
import jax, jax.numpy as jnp
from jax import lax
import numpy as np

D_MODEL = 4096
BATCH = 2
SEQ = 4096
DEPTH = 2

D_MIX = D_MODEL
W_CONV = D_MIX // 2
CONV_GROUPS = 16
CONV_K = 3
N_HEADS = 16
HEAD_DIM = (D_MIX - W_CONV) // N_HEADS
N_KV = 4
GQA = N_HEADS // N_KV
W_ATT = N_HEADS * HEAD_DIM
W_KV = N_KV * HEAD_DIM
L_CMP = 32
D_CMP = 16
L_SLC = 64
N_SEL = 16
N_LOCAL = 2
WIN = 512
Q_BLOCK = 128
SEL_Q_CHUNK = 32
N_BRANCH = 3
DEEPNORM_ALPHA = (2 * DEPTH) ** 0.25
DEEPNORM_BETA = (8 * DEPTH) ** -0.25
LN_EPS = 1e-5
NEG = -1e30
BIG = 1e9

SPLIT_SIZES = (W_CONV, W_CONV, W_CONV, W_CONV,
               W_ATT,
               W_KV, W_KV, W_KV, W_KV, W_KV, W_KV,
               W_ATT,
               N_HEADS * N_BRANCH)
N_IN = sum(SPLIT_SIZES)

kernel_name = "hymba_conv_nsa_deepnorm"


def layer_norm(x, g, b):
    x32 = x.astype(jnp.float32)
    mu = x32.mean(-1, keepdims=True)
    var = jnp.square(x32 - mu).mean(-1, keepdims=True)
    y = (x32 - mu) * lax.rsqrt(var + LN_EPS) * g.astype(jnp.float32) + b.astype(jnp.float32)
    return y.astype(x.dtype)


def short_conv_mixer(h, bg, cg, w):
    u = cg * h
    y = lax.conv_general_dilated(u, w[:, None, :].astype(u.dtype), window_strides=(1,),
                                 padding=[(CONV_K - 1, 0)],
                                 dimension_numbers=('NWC', 'WIO', 'NWC'),
                                 feature_group_count=W_CONV)
    return bg * y


def compress_blocks(kv, pos, w1, w2):
    b, t, g, d = kv.shape
    n_chunk = t // D_CMP
    r = L_CMP // D_CMP
    n_cmp = n_chunk - r + 1
    chunks = kv.reshape(b, n_chunk, D_CMP, g, d)
    blocks = jnp.concatenate([chunks[:, i:i + n_cmp] for i in range(r)], axis=2)
    blocks = blocks + pos[None, None, :, None, :]
    flat = blocks.transpose(0, 1, 3, 2, 4).reshape(b, n_cmp, g, L_CMP * d)
    hid = jax.nn.silu(jnp.einsum('bngf,fe->bnge', flat, w1))
    return jnp.einsum('bnge,ed->bngd', hid, w2)


def band_slabs(x):
    b, t, g, d = x.shape
    n_qb = t // Q_BLOCK
    n_wb = WIN // Q_BLOCK
    xp = jnp.pad(x, ((0, 0), (WIN, 0), (0, 0), (0, 0))).reshape(b, n_qb + n_wb, Q_BLOCK, g, d)
    return jnp.concatenate([xp[:, i:i + n_qb] for i in range(n_wb + 1)], axis=2)


def nsa_mixer(q, k_cmp, v_cmp, k_slc, v_slc, k_win, v_win, gate_logits,
              pos_k, w1_k, w2_k, pos_v, w1_v, w2_v):
    b, t, _ = q.shape
    scale = 1.0 / math.sqrt(HEAD_DIM)
    f32 = jnp.float32
    q5 = q.reshape(b, t, N_KV, GQA, HEAD_DIM)
    kv4 = lambda a: a.reshape(b, t, N_KV, HEAD_DIM)
    tpos = jnp.arange(t)

    kc = compress_blocks(kv4(k_cmp), pos_k, w1_k, w2_k)
    vc = compress_blocks(kv4(v_cmp), pos_v, w1_v, w2_v)
    n_cmp = kc.shape[1]
    cmp_end = jnp.arange(n_cmp) * D_CMP + L_CMP - 1
    cmp_valid = cmp_end[None, :] <= tpos[:, None]
    s = jnp.einsum('btgrd,bngd->bgrtn', q5, kc).astype(f32) * scale
    p_cmp = jax.nn.softmax(jnp.where(cmp_valid, s, NEG), axis=-1) * cmp_valid
    o_cmp = jnp.einsum('bgrtn,bngd->btgrd', p_cmp.astype(vc.dtype), vc)

    n_slc = t // L_SLC
    ci = jnp.arange(n_cmp)[:, None]
    sj = jnp.arange(n_slc)[None, :]
    overlap = ((ci * D_CMP < (sj + 1) * L_SLC) & (ci * D_CMP + L_CMP > sj * L_SLC)).astype(f32)
    p_slc = jnp.einsum('bgrtn,nj->bgtj', p_cmp, overlap)
    cur = tpos // L_SLC
    jj = jnp.arange(n_slc)
    block_valid = jj[None, :] <= cur[:, None]
    forced = (jj == 0)[None, :] | (block_valid & (jj[None, :] > cur[:, None] - N_LOCAL))
    sel_score = jnp.where(forced, BIG, jnp.where(block_valid, p_slc, -1.0))
    n_sel = min(N_SEL, n_slc)
    _, sel_idx = lax.top_k(sel_score, n_sel)

    kb = kv4(k_slc).reshape(b, n_slc, L_SLC, N_KV, HEAD_DIM).transpose(0, 3, 1, 2, 4)
    vb = kv4(v_slc).reshape(b, n_slc, L_SLC, N_KV, HEAD_DIM).transpose(0, 3, 1, 2, 4)
    bi = jnp.arange(b)[:, None, None, None]
    gi = jnp.arange(N_KV)[None, :, None, None]
    n_ch = t // SEL_Q_CHUNK
    q_ch = q5.reshape(b, n_ch, SEL_Q_CHUNK, N_KV, GQA, HEAD_DIM).transpose(1, 0, 2, 3, 4, 5)
    i_ch = sel_idx.reshape(b, N_KV, n_ch, SEL_Q_CHUNK, n_sel).transpose(2, 0, 1, 3, 4)
    t_ch = tpos.reshape(n_ch, SEL_Q_CHUNK)

    def sel_chunk(args):
        qc, ic, tc = args
        kg = kb[bi, gi, ic]
        vg = vb[bi, gi, ic]
        kpos = ic[..., None] * L_SLC + jnp.arange(L_SLC)
        mask = kpos <= tc[None, None, :, None, None]
        sc = jnp.einsum('bqgrd,bgqnld->bgrqnl', qc, kg).astype(f32) * scale
        sc = jnp.where(mask[:, :, None], sc, NEG).reshape(b, N_KV, GQA, SEL_Q_CHUNK, n_sel * L_SLC)
        pc = jax.nn.softmax(sc, axis=-1).astype(vg.dtype)
        return jnp.einsum('bgrqk,bgqkd->bqgrd', pc,
                          vg.reshape(b, N_KV, SEL_Q_CHUNK, n_sel * L_SLC, HEAD_DIM))

    o_slc = lax.map(sel_chunk, (q_ch, i_ch, t_ch))
    o_slc = o_slc.transpose(1, 0, 2, 3, 4, 5).reshape(b, t, N_KV, GQA, HEAD_DIM)

    n_qb = t // Q_BLOCK
    kw = band_slabs(kv4(k_win))
    vw = band_slabs(kv4(v_win))
    qb = q5.reshape(b, n_qb, Q_BLOCK, N_KV, GQA, HEAD_DIM)
    n_key = WIN + Q_BLOCK
    qpos = jnp.arange(n_qb)[:, None, None] * Q_BLOCK + jnp.arange(Q_BLOCK)[None, :, None]
    kpos = jnp.arange(n_qb)[:, None, None] * Q_BLOCK - WIN + jnp.arange(n_key)[None, None, :]
    wmask = (kpos <= qpos) & (kpos > qpos - WIN) & (kpos >= 0)
    sw = jnp.einsum('bnqgrd,bnkgd->bngrqk', qb, kw).astype(f32) * scale
    pw = jax.nn.softmax(jnp.where(wmask[None, :, None, None], sw, NEG), axis=-1).astype(vw.dtype)
    o_win = jnp.einsum('bngrqk,bnkgd->bnqgrd', pw, vw).reshape(b, t, N_KV, GQA, HEAD_DIM)

    gates = jax.nn.sigmoid(gate_logits.reshape(b, t, N_KV, GQA, N_BRANCH))
    o = gates[..., 0:1] * o_cmp + gates[..., 1:2] * o_slc + gates[..., 2:3] * o_win
    return o.reshape(b, t, W_ATT)


def hybrid_layer(x, w_in, conv_w, pos_k, w1_k, w2_k, pos_v, w1_v, w2_v, w_out, ln_g, ln_b):
    proj = jnp.einsum('btd,df->btf', x, w_in)
    split_at = [int(v) for v in np.cumsum(SPLIT_SIZES)[:-1]]
    (h, bg, cg, z_conv, q, k_cmp, v_cmp, k_slc, v_slc, k_win, v_win,
     z_att, gate_logits) = jnp.split(proj, split_at, axis=-1)
    y_conv = short_conv_mixer(h, bg, cg, conv_w) * jax.nn.silu(z_conv)
    y_att = nsa_mixer(q, k_cmp, v_cmp, k_slc, v_slc, k_win, v_win, gate_logits,
                      pos_k, w1_k, w2_k, pos_v, w1_v, w2_v) * jax.nn.silu(z_att)
    mixed = jnp.concatenate([y_conv, y_att], axis=-1)
    out = jnp.einsum('btf,fd->btd', mixed, w_out)
    return layer_norm(DEEPNORM_ALPHA * x + out, ln_g, ln_b)


def setup_inputs(seed: int = 0) -> dict:
    key = jax.random.key(seed)
    ks = jax.random.split(key, 12)
    f32 = jnp.float32

    def nrm(k, shape, s):
        return jax.random.normal(k, shape, f32) * s

    return {
        "x": nrm(ks[0], (BATCH, SEQ, D_MODEL), 1.0),
        "w_in": nrm(ks[1], (DEPTH, D_MODEL, N_IN), D_MODEL ** -0.5),
        "conv_w": nrm(ks[2], (DEPTH, CONV_K, W_CONV), CONV_K ** -0.5),
        "cmp_pos_k": nrm(ks[3], (DEPTH, L_CMP, HEAD_DIM), 0.2),
        "cmp_w1_k": nrm(ks[4], (DEPTH, L_CMP * HEAD_DIM, HEAD_DIM), (L_CMP * HEAD_DIM) ** -0.5),
        "cmp_w2_k": nrm(ks[5], (DEPTH, HEAD_DIM, HEAD_DIM), HEAD_DIM ** -0.5),
        "cmp_pos_v": nrm(ks[6], (DEPTH, L_CMP, HEAD_DIM), 0.2),
        "cmp_w1_v": nrm(ks[7], (DEPTH, L_CMP * HEAD_DIM, HEAD_DIM), (L_CMP * HEAD_DIM) ** -0.5),
        "cmp_w2_v": nrm(ks[8], (DEPTH, HEAD_DIM, HEAD_DIM), HEAD_DIM ** -0.5),
        "w_out": nrm(ks[9], (DEPTH, D_MIX, D_MODEL), D_MIX ** -0.5 * DEEPNORM_BETA),
        "ln_g": 1.0 + nrm(ks[10], (DEPTH, D_MODEL), 0.02),
        "ln_b": nrm(ks[11], (DEPTH, D_MODEL), 0.02),
    }


def reference(x, w_in, conv_w, cmp_pos_k, cmp_w1_k, cmp_w2_k, cmp_pos_v, cmp_w1_v, cmp_w2_v,
              w_out, ln_g, ln_b):
    for l in range(DEPTH):
        x = hybrid_layer(x, w_in[l], conv_w[l], cmp_pos_k[l], cmp_w1_k[l], cmp_w2_k[l],
                         cmp_pos_v[l], cmp_w1_v[l], cmp_w2_v[l], w_out[l], ln_g[l], ln_b[l])
    return x
```

```python
import functools
import math

import jax
import jax.numpy as jnp
from jax import lax
from jax.experimental import pallas as pl
from jax.experimental.pallas import tpu as pltpu

D_MODEL = 4096
DEPTH = 2
W_CONV = D_MODEL // 2
CONV_K = 3
N_HEADS = 16
HEAD_DIM = 128
N_KV = 4
GQA = N_HEADS // N_KV
W_ATT = N_HEADS * HEAD_DIM
W_KV = N_KV * HEAD_DIM
L_CMP = 32
D_CMP = 16
L_SLC = 64
N_SEL = 16
N_LOCAL = 2
WIN = 512
N_BRANCH = 3
DEEPNORM_ALPHA = (2 * DEPTH) ** 0.25
LN_EPS = 1e-5
NEG = -1e30
BIG = 1e9
SCALE = 1.0 / math.sqrt(HEAD_DIM)

COL_H = 0
COL_BG = COL_H + W_CONV
COL_CG = COL_BG + W_CONV
COL_ZC = COL_CG + W_CONV
COL_Q = COL_ZC + W_CONV
COL_KCMP = COL_Q + W_ATT
COL_VCMP = COL_KCMP + W_KV
COL_KSLC = COL_VCMP + W_KV
COL_VSLC = COL_KSLC + W_KV
COL_KWIN = COL_VSLC + W_KV
COL_VWIN = COL_KWIN + W_KV
COL_ZATT = COL_VWIN + W_KV
COL_GATE = COL_ZATT + W_ATT
N_GATE = N_HEADS * N_BRANCH

LANES = 128
SUBLANES = 8
VMEM_LIMIT_BYTES = 56 * 1024 * 1024

PROJ_TM, PROJ_TN, PROJ_TK = 1024, 1536, 1024
GATE_TM = 512
OUT_TM, OUT_TK = 256, 512
Q_BLK = 128
K_TILE = 512
ROWS = GQA * Q_BLK

F32 = jnp.float32
BF16 = jnp.bfloat16


def _params(*semantics):
    return pltpu.CompilerParams(dimension_semantics=semantics, vmem_limit_bytes=VMEM_LIMIT_BYTES)


def _sigmoid(v):
    return 1.0 / (1.0 + jnp.exp(-v))


def _dot_nt(a, b):
    return lax.dot_general(a, b, (((1,), (1,)), ((), ())), preferred_element_type=F32)


def _matmul_kernel(x_ref, w_ref, o_ref, acc_ref):
    k = pl.program_id(2)

    @pl.when(k == 0)
    def _():
        acc_ref[...] = jnp.zeros_like(acc_ref)

    acc_ref[...] += jnp.dot(x_ref[...], w_ref[...], preferred_element_type=F32)

    @pl.when(k == pl.num_programs(2) - 1)
    def _():
        o_ref[...] = acc_ref[...].astype(o_ref.dtype)


def _proj(xb, wb):
    m, kdim = xb.shape
    n = wb.shape[1]
    tm = min(PROJ_TM, m)
    return pl.pallas_call(
        _matmul_kernel,
        grid=(m // tm, n // PROJ_TN, kdim // PROJ_TK),
        in_specs=[pl.BlockSpec((tm, PROJ_TK), lambda i, j, k: (i, k)),
                  pl.BlockSpec((PROJ_TK, PROJ_TN), lambda i, j, k: (k, j))],
        out_specs=pl.BlockSpec((tm, PROJ_TN), lambda i, j, k: (i, j)),
        out_shape=jax.ShapeDtypeStruct((m, n), BF16),
        scratch_shapes=[pltpu.VMEM((tm, PROJ_TN), F32)],
        compiler_params=_params("parallel", "parallel", "arbitrary"),
        name="proj",
    )(xb, wb)


def _gate_kernel(x_ref, w_ref, o_ref):
    o_ref[0] = jnp.dot(x_ref[...], w_ref[0], preferred_element_type=F32)


def _gate_logits(xb, wg):
    m, kdim = xb.shape
    tm = min(GATE_TM, m)
    return pl.pallas_call(
        _gate_kernel,
        grid=(m // tm, N_KV),
        in_specs=[pl.BlockSpec((tm, kdim), lambda i, g: (i, 0)),
                  pl.BlockSpec((1, kdim, LANES), lambda i, g: (g, 0, 0))],
        out_specs=pl.BlockSpec((1, tm, LANES), lambda i, g: (g, i, 0)),
        out_shape=jax.ShapeDtypeStruct((N_KV, m, LANES), F32),
        compiler_params=_params("parallel", "arbitrary"),
        name="gates",
    )(xb, wg)


def _conv_kernel(h_ref, bg_ref, cg_ref, z_ref, w_ref, o_ref, u_scr, *, t):
    u_scr[0:SUBLANES, :] = jnp.zeros((SUBLANES, LANES), F32)
    u_scr[SUBLANES:SUBLANES + t, :] = cg_ref[0].astype(F32) * h_ref[0].astype(F32)
    w = w_ref[...]
    conv = (w[0:1, :] * u_scr[SUBLANES - 2:SUBLANES - 2 + t, :]
            + w[1:2, :] * u_scr[SUBLANES - 1:SUBLANES - 1 + t, :]
            + w[2:3, :] * u_scr[SUBLANES:SUBLANES + t, :])
    z = z_ref[0].astype(F32)
    o_ref[0] = (bg_ref[0].astype(F32) * conv * (z * _sigmoid(z))).astype(o_ref.dtype)


def _conv_mixer(proj3, conv_w):
    b, t, _ = proj3.shape
    nblk = W_CONV // LANES

    def col(off):
        return pl.BlockSpec((1, t, LANES), lambda bi, c: (bi, 0, off // LANES + c))

    return pl.pallas_call(
        functools.partial(_conv_kernel, t=t),
        grid=(b, nblk),
        in_specs=[col(COL_H), col(COL_BG), col(COL_CG), col(COL_ZC),
                  pl.BlockSpec((CONV_K, LANES), lambda bi, c: (0, c))],
        out_specs=pl.BlockSpec((1, t, LANES), lambda bi, c: (bi, 0, c)),
        out_shape=jax.ShapeDtypeStruct((b, t, W_CONV), BF16),
        scratch_shapes=[pltpu.VMEM((t + SUBLANES, LANES), F32)],
        compiler_params=_params("parallel", "parallel"),
        name="conv",
    )(proj3, proj3, proj3, proj3, conv_w)


def _cmp_kernel(k_ref, v_ref, posk_ref, w1k_ref, w2k_ref, posv_ref, w1v_ref, w2v_ref,
                kc_ref, vc_ref, scr, *, n_chunk):
    half = L_CMP // D_CMP
    assert half == 2
    row = lax.broadcasted_iota(jnp.int32, (n_chunk, 1), 0)

    def compress(src_ref, pos_ref, w1_ref, w2_ref, dst_ref):
        scr[...] = src_ref[0].astype(F32)
        xs = [scr[pl.ds(l, n_chunk, stride=D_CMP), :] for l in range(D_CMP)]
        pos = pos_ref[...]
        flat_a = jnp.concatenate([(xs[l] + pos[l:l + 1, :]).astype(BF16) for l in range(D_CMP)], axis=1)
        flat_b = jnp.concatenate([(xs[l] + pos[D_CMP + l:D_CMP + l + 1, :]).astype(BF16)
                                  for l in range(D_CMP)], axis=1)
        kh = D_CMP * HEAD_DIM
        a = jnp.dot(flat_a, w1_ref[0:kh, :], preferred_element_type=F32)
        bm = jnp.dot(flat_b, w1_ref[kh:2 * kh, :], preferred_element_type=F32)
        hid = a + pltpu.roll(bm, n_chunk - 1, axis=0)
        hid = hid * _sigmoid(hid)
        out = jnp.dot(hid.astype(BF16), w2_ref[...], preferred_element_type=F32)
        dst_ref[0, 0] = jnp.where(row < n_chunk - 1, out, 0.0).astype(dst_ref.dtype)

    compress(k_ref, posk_ref, w1k_ref, w2k_ref, kc_ref)
    compress(v_ref, posv_ref, w1v_ref, w2v_ref, vc_ref)


def _compress(proj3, pos_k, w1_k, w2_k, pos_v, w1_v, w2_v):
    b, t, _ = proj3.shape
    n_chunk = t // D_CMP

    def col(off):
        return pl.BlockSpec((1, t, HEAD_DIM), lambda bi, g: (bi, 0, off // HEAD_DIM + g))

    def full(a):
        return pl.BlockSpec(a.shape, lambda bi, g: (0,) * a.ndim)

    out_spec = pl.BlockSpec((1, 1, n_chunk, HEAD_DIM), lambda bi, g: (bi, g, 0, 0))
    out_shape = jax.ShapeDtypeStruct((b, N_KV, n_chunk, HEAD_DIM), BF16)
    return pl.pallas_call(
        functools.partial(_cmp_kernel, n_chunk=n_chunk),
        grid=(b, N_KV),
        in_specs=[col(COL_KCMP), col(COL_VCMP), full(pos_k), full(w1_k), full(w2_k),
                  full(pos_v), full(w1_v), full(w2_v)],
        out_specs=[out_spec, out_spec],
        out_shape=[out_shape, out_shape],
        scratch_shapes=[pltpu.VMEM((t, HEAD_DIM), F32)],
        compiler_params=_params("parallel", "parallel"),
        name="cmp",
    )(proj3, proj3, pos_k, w1_k, w2_k, pos_v, w1_v, w2_v)


def _nsa_kernel(q_ref, z_ref, gl_ref, kc_ref, vc_ref, ks_ref, vs_ref, kw_ref, vw_ref, ov_ref, e_ref,
                o_ref, bias_scr, m_scr, l_scr, acc_scr, *, n_slc, n_kt):
    qb = pl.program_id(2)
    t0 = qb * Q_BLK
    q = q_ref[0]
    qs = jnp.concatenate([q[:, r * HEAD_DIM:(r + 1) * HEAD_DIM] for r in range(GQA)], axis=0)
    row = lax.broadcasted_iota(jnp.int32, (ROWS, 1), 0)
    t_row = t0 + (row & (Q_BLK - 1))
    t_q = t0 + lax.broadcasted_iota(jnp.int32, (Q_BLK, 1), 0)

    kc = kc_ref[0, 0]
    vc = vc_ref[0, 0]
    n_c = kc.shape[0]
    s = _dot_nt(qs, kc) * SCALE
    n_i = lax.broadcasted_iota(jnp.int32, (1, n_c), 1)
    valid = (n_i * D_CMP + (L_CMP - 1)) <= t_row
    sm = jnp.where(valid, s, NEG)
    m = jnp.max(sm, axis=-1, keepdims=True)
    e = jnp.where(valid, jnp.exp(sm - m), 0.0)
    l = jnp.sum(e, axis=-1, keepdims=True)
    p = e / jnp.where(l > 0.0, l, 1.0)
    o_cmp = jnp.dot(p.astype(BF16), vc, preferred_element_type=F32)

    psum = p[0:Q_BLK]
    for r in range(1, GQA):
        psum = psum + p[r * Q_BLK:(r + 1) * Q_BLK]
    hi = psum.astype(BF16)
    lo = (psum - hi.astype(F32)).astype(BF16)
    ov = ov_ref[...]
    pslc = (jnp.dot(hi, ov, preferred_element_type=F32)
            + jnp.dot(lo, ov, preferred_element_type=F32))

    j = lax.broadcasted_iota(jnp.int32, (Q_BLK, LANES), 1)
    cur = t_q >> int(math.log2(L_SLC))
    score = jnp.where(j <= cur, jnp.where(j > cur - N_LOCAL, BIG, pslc), -1.0)
    score = jnp.where(j == 0, BIG, score)
    score = jnp.where(j < n_slc, score, -2.0)
    sc_t = score.T

    n_grp = n_slc // SUBLANES
    groups = [sc_t[g * SUBLANES:(g + 1) * SUBLANES, :] for g in range(n_grp)]
    ranks = [jnp.zeros((SUBLANES, Q_BLK), F32) for _ in range(n_grp)]
    sub = lax.broadcasted_iota(jnp.int32, (SUBLANES, Q_BLK), 0)
    for i in range(n_slc):
        gi, ii = divmod(i, SUBLANES)
        ri = jnp.broadcast_to(groups[gi][ii:ii + 1, :], (SUBLANES, Q_BLK))
        for g in range(n_grp):
            if g < gi:
                inc = jnp.where(ri > groups[g], 1.0, 0.0)
            elif g > gi:
                inc = jnp.where(ri >= groups[g], 1.0, 0.0)
            else:
                tie = jnp.where(sub > ii, jnp.where(ri == groups[g], 1.0, 0.0), 0.0)
                inc = jnp.where(ri > groups[g], 1.0, 0.0) + tie
            ranks[g] = ranks[g] + inc
    sel_t = jnp.concatenate([jnp.where(rk < float(N_SEL), 1.0, 0.0) for rk in ranks]
                            + [jnp.zeros((LANES - n_slc, Q_BLK), F32)], axis=0)
    sel = sel_t.T.astype(BF16)

    n_tiles = qb // (K_TILE // Q_BLK) + 1
    for kt in range(n_kt):
        @pl.when(kt < n_tiles)
        def _(kt=kt):
            mk = jnp.dot(sel, e_ref[:, kt * K_TILE:(kt + 1) * K_TILE], preferred_element_type=F32)
            kpos = kt * K_TILE + lax.broadcasted_iota(jnp.int32, (1, K_TILE), 1)
            bias_scr[kt] = jnp.where(kpos <= t_q, jnp.where(mk > 0.5, 0.0, NEG), NEG)

    m_scr[...] = jnp.full_like(m_scr, NEG)
    l_scr[...] = jnp.zeros_like(l_scr)
    acc_scr[...] = jnp.zeros_like(acc_scr)

    def slc_body(kt, carry):
        ko = pl.multiple_of(kt * K_TILE, K_TILE)
        k = ks_ref[0, pl.ds(ko, K_TILE), :]
        v = vs_ref[0, pl.ds(ko, K_TILE), :]
        bias = bias_scr[kt]
        sk = _dot_nt(qs, k) * SCALE + jnp.concatenate([bias] * GQA, axis=0)
        m_old = m_scr[...]
        m_new = jnp.maximum(m_old, jnp.max(sk, axis=-1, keepdims=True))
        alpha = jnp.exp(m_old - m_new)
        pk = jnp.exp(sk - m_new)
        l_scr[...] = alpha * l_scr[...] + jnp.sum(pk, axis=-1, keepdims=True)
        acc_scr[...] = alpha * acc_scr[...] + jnp.dot(pk.astype(BF16), v, preferred_element_type=F32)
        m_scr[...] = m_new
        return carry

    lax.fori_loop(0, n_tiles, slc_body, 0)
    o_slc = acc_scr[...] / l_scr[...]

    n_win = WIN + Q_BLK
    ws = pl.multiple_of(jnp.maximum(qb - WIN // Q_BLK, 0) * Q_BLK, Q_BLK)
    kw = kw_ref[0, pl.ds(ws, n_win), :]
    vw = vw_ref[0, pl.ds(ws, n_win), :]
    sw = _dot_nt(qs, kw) * SCALE
    kpos = ws + lax.broadcasted_iota(jnp.int32, (1, n_win), 1)
    sw = jnp.where(kpos <= t_row, jnp.where(kpos > t_row - WIN, sw, NEG), NEG)
    mw = jnp.max(sw, axis=-1, keepdims=True)
    pw = jnp.exp(sw - mw)
    lw = jnp.sum(pw, axis=-1, keepdims=True)
    o_win = jnp.dot(pw.astype(BF16), vw, preferred_element_type=F32) / lw

    gs = _sigmoid(gl_ref[0])
    z = z_ref[0].astype(F32)
    for r in range(GQA):
        rows = slice(r * Q_BLK, (r + 1) * Q_BLK)
        cols = slice(r * HEAD_DIM, (r + 1) * HEAD_DIM)
        c0 = r * N_BRANCH
        o = (gs[:, c0:c0 + 1] * o_cmp[rows] + gs[:, c0 + 1:c0 + 2] * o_slc[rows]
             + gs[:, c0 + 2:c0 + 3] * o_win[rows])
        zr = z[:, cols]
        o_ref[0, :, cols] = (o * (zr * _sigmoid(zr))).astype(o_ref.dtype)


def _nsa(proj3, gate_l, kc, vc):
    b, t, _ = proj3.shape
    n_qb = t // Q_BLK
    n_slc = t // L_SLC
    n_kt = t // K_TILE
    n_c = t // D_CMP
    assert n_slc <= LANES and n_slc % SUBLANES == 0 and n_slc >= N_SEL and t >= WIN + Q_BLK

    ci = jnp.arange(n_c)[:, None]
    sj = jnp.arange(LANES)[None, :]
    overlap = ((ci * D_CMP < (sj + 1) * L_SLC) & (ci * D_CMP + L_CMP > sj * L_SLC)
               & (ci < n_c - 1) & (sj < n_slc)).astype(BF16)
    expand = (jnp.arange(LANES)[:, None] == (jnp.arange(t)[None, :] // L_SLC)).astype(BF16)

    gq = GQA * HEAD_DIM

    def kv(off):
        return pl.BlockSpec((1, t, HEAD_DIM), lambda bi, g, i: (bi, 0, off // HEAD_DIM + g))

    cmp_spec = pl.BlockSpec((1, 1, n_c, HEAD_DIM), lambda bi, g, i: (bi, g, 0, 0))
    return pl.pallas_call(
        functools.partial(_nsa_kernel, n_slc=n_slc, n_kt=n_kt),
        grid=(b, N_KV, n_qb),
        in_specs=[pl.BlockSpec((1, Q_BLK, gq), lambda bi, g, i: (bi, i, COL_Q // gq + g)),
                  pl.BlockSpec((1, Q_BLK, gq), lambda bi, g, i: (bi, i, COL_ZATT // gq + g)),
                  pl.BlockSpec((1, Q_BLK, LANES), lambda bi, g, i: (g, bi * n_qb + i, 0)),
                  cmp_spec, cmp_spec,
                  kv(COL_KSLC), kv(COL_VSLC), kv(COL_KWIN), kv(COL_VWIN),
                  pl.BlockSpec((n_c, LANES), lambda bi, g, i: (0, 0)),
                  pl.BlockSpec((LANES, t), lambda bi, g, i: (0, 0))],
        out_specs=pl.BlockSpec((1, Q_BLK, gq), lambda bi, g, i: (bi, i, g)),
        out_shape=jax.ShapeDtypeStruct((b, t, W_ATT), BF16),
        scratch_shapes=[pltpu.VMEM((n_kt, Q_BLK, K_TILE), F32),
                        pltpu.VMEM((ROWS, 1), F32),
                        pltpu.VMEM((ROWS, 1), F32),
                        pltpu.VMEM((ROWS, HEAD_DIM), F32)],
        compiler_params=_params("parallel", "parallel", "arbitrary"),
        name="nsa",
    )(proj3, proj3, gate_l, kc, vc, proj3, proj3, proj3, proj3, overlap, expand)


def _out_ln_kernel(yc_ref, ya_ref, w_ref, x_ref, g_ref, b_ref, o_ref, ob_ref, acc_ref):
    k = pl.program_id(1)
    nk = pl.num_programs(1)

    @pl.when(k == 0)
    def _():
        acc_ref[...] = jnp.zeros_like(acc_ref)

    @pl.when(k < nk // 2)
    def _():
        acc_ref[...] += jnp.dot(yc_ref[...], w_ref[...], preferred_element_type=F32)

    @pl.when(k >= nk // 2)
    def _():
        acc_ref[...] += jnp.dot(ya_ref[...], w_ref[...], preferred_element_type=F32)

    @pl.when(k == nk - 1)
    def _():
        v = DEEPNORM_ALPHA * x_ref[...] + acc_ref[...]
        mu = jnp.mean(v, axis=-1, keepdims=True)
        d = v - mu
        var = jnp.mean(d * d, axis=-1, keepdims=True)
        y = d * lax.rsqrt(var + LN_EPS) * g_ref[...] + b_ref[...]
        o_ref[...] = y
        ob_ref[...] = y.astype(ob_ref.dtype)


def _out_ln(y_conv, y_att, w_out_b, x2, ln_g, ln_b):
    m, d = x2.shape
    tm = min(OUT_TM, m)
    nk = (W_CONV + W_ATT) // OUT_TK
    nh = nk // 2
    return pl.pallas_call(
        _out_ln_kernel,
        grid=(m // tm, nk),
        in_specs=[pl.BlockSpec((tm, OUT_TK), lambda i, k: (i, jnp.minimum(k, nh - 1))),
                  pl.BlockSpec((tm, OUT_TK), lambda i, k: (i, jnp.maximum(k - nh, 0))),
                  pl.BlockSpec((OUT_TK, d), lambda i, k: (k, 0)),
                  pl.BlockSpec((tm, d), lambda i, k: (i, 0)),
                  pl.BlockSpec((1, d), lambda i, k: (0, 0)),
                  pl.BlockSpec((1, d), lambda i, k: (0, 0))],
        out_specs=[pl.BlockSpec((tm, d), lambda i, k: (i, 0)),
                   pl.BlockSpec((tm, d), lambda i, k: (i, 0))],
        out_shape=[jax.ShapeDtypeStruct((m, d), F32), jax.ShapeDtypeStruct((m, d), BF16)],
        scratch_shapes=[pltpu.VMEM((tm, d), F32)],
        compiler_params=_params("parallel", "arbitrary"),
        name="out_ln",
    )(y_conv, y_att, w_out_b, x2, ln_g, ln_b)


def _layer(x2, xb, b, t, w_in, conv_w, pos_k, w1_k, w2_k, pos_v, w1_v, w2_v, w_out, ln_g, ln_b):
    m = b * t
    w_main = w_in[:, :COL_GATE].astype(BF16)
    wg = w_in[:, COL_GATE:].reshape(D_MODEL, N_KV, GQA * N_BRANCH).transpose(1, 0, 2)
    wg = jnp.pad(wg, ((0, 0), (0, 0), (0, LANES - GQA * N_BRANCH))).astype(BF16)

    proj = _proj(xb, w_main)
    gate_l = _gate_logits(xb, wg)
    proj3 = proj.reshape(b, t, COL_GATE)
    y_conv = _conv_mixer(proj3, conv_w)
    kc, vc = _compress(proj3, pos_k, w1_k.astype(BF16), w2_k.astype(BF16),
                       pos_v, w1_v.astype(BF16), w2_v.astype(BF16))
    y_att = _nsa(proj3, gate_l, kc, vc)
    return _out_ln(y_conv.reshape(m, W_CONV), y_att.reshape(m, W_ATT), w_out.astype(BF16),
                   x2, ln_g.reshape(1, D_MODEL), ln_b.reshape(1, D_MODEL))


def kernel(x, w_in, conv_w, cmp_pos_k, cmp_w1_k, cmp_w2_k, cmp_pos_v, cmp_w1_v, cmp_w2_v, w_out, ln_g, ln_b):
    b, t, d = x.shape
    assert d == D_MODEL and w_in.shape[0] == DEPTH
    x2 = x.reshape(b * t, d)
    xb = x2.astype(BF16)
    for l in range(DEPTH):
        x2, xb = _layer(x2, xb, b, t, w_in[l], conv_w[l], cmp_pos_k[l], cmp_w1_k[l], cmp_w2_k[l],
                        cmp_pos_v[l], cmp_w1_v[l], cmp_w2_v[l], w_out[l], ln_g[l], ln_b[l])
    return x2.reshape(b, t, d)
```
